```python
import jax, jax.numpy as jnp
from jax import lax
import numpy as np

D_MODEL = 1024
BATCH = 4
SEQ = 8192
DEPTH = 1

PLE_DIM = 256
GLA_HEADS = 4
GLA_DK = 64
GLA_DV = 128
GLA_KEY = GLA_HEADS * GLA_DK
GLA_VAL = GLA_HEADS * GLA_DV
GLA_RANK = 16
GLA_TAU = 16.0
GLA_CHUNK = 64
CONV_DIM = 512
CONV_GROUPS = 8
CONV_K = 3
D_MIX = GLA_VAL + CONV_DIM
IN_SPLIT_SIZES = (GLA_KEY, GLA_KEY, GLA_VAL, GLA_VAL, GLA_RANK, CONV_DIM, CONV_DIM, CONV_DIM)
D_IN = sum(IN_SPLIT_SIZES)
D_FF = 2816
FFN_K = 3
EPS = 1e-6

kernel_name = "hybrid_gla_shortconv_convffn_ple"


def rmsnorm(x, g):
    xf = x.astype(jnp.float32)
    y = xf * lax.rsqrt(jnp.mean(xf * xf, axis=-1, keepdims=True) + EPS)
    return (y * g.astype(jnp.float32)).astype(x.dtype)


def causal_dwconv(u, w):
    K = w.shape[0]
    S = u.shape[1]
    up = jnp.pad(u, ((0, 0), (K - 1, 0), (0, 0)))
    out = w[K - 1] * u
    for k in range(K - 1):
        out = out + w[k] * up[:, k:k + S]
    return out


def gla_chunked(q, k, v, log_a):
    Bn, S, H, dk = q.shape
    dv = v.shape[-1]
    N = S // GLA_CHUNK

    def to_chunks(t):
        return t.astype(jnp.float32).reshape(Bn, N, GLA_CHUNK, H, t.shape[-1]).transpose(0, 3, 1, 2, 4)

    qc = to_chunks(q) * (dk ** -0.5)
    kc = to_chunks(k)
    vc = to_chunks(v)
    gc = to_chunks(log_a)
    b = jnp.cumsum(gc, axis=-2)
    b_last = b[..., -1:, :]
    q_d = qc * jnp.exp(b)
    k_d = kc * jnp.exp(-b)
    k_end = kc * jnp.exp(b_last - b)

    causal = jnp.tril(jnp.ones((GLA_CHUNK, GLA_CHUNK), dtype=bool))
    scores = jnp.einsum('bhnid,bhnjd->bhnij', q_d, k_d)
    scores = jnp.where(causal, scores, 0.0)
    o_intra = jnp.einsum('bhnij,bhnjv->bhniv', scores, vc)

    U = jnp.einsum('bhnjd,bhnjv->bhndv', k_end, vc)
    decay = jnp.exp(b_last[..., 0, :])

    def step(state, inp):
        dec_n, u_n = inp
        new = dec_n[..., None] * state + u_n
        return new, state

    s0 = jnp.zeros((Bn, H, dk, dv), jnp.float32)
    _, s_prev = lax.scan(step, s0, (jnp.moveaxis(decay, 2, 0), jnp.moveaxis(U, 2, 0)))
    s_prev = jnp.moveaxis(s_prev, 0, 2)
    o_inter = jnp.einsum('bhnid,bhndv->bhniv', q_d, s_prev)
    o = o_intra + o_inter
    return o.transpose(0, 2, 3, 1, 4).reshape(Bn, S, H, dv)


def setup_inputs(seed: int = 0) -> dict:
    key = jax.random.key(seed)
    ks = jax.random.split(key, 20)
    f32 = jnp.float32

    def w(k, shape, fan_in):
        return jax.random.normal(k, shape, f32) * (fan_in ** -0.5)

    def gain(k, shape):
        return 1.0 + 0.02 * jax.random.normal(k, shape, f32)

    return {
        "x": jax.random.normal(ks[0], (BATCH, SEQ, D_MODEL), f32),
        "p": jax.random.normal(ks[1], (DEPTH, BATCH, SEQ, PLE_DIM), f32),
        "attn_norm": gain(ks[2], (DEPTH, D_MODEL)),
        "w_in": w(ks[3], (DEPTH, D_MODEL, D_IN), D_MODEL),
        "w_alpha_up": w(ks[4], (DEPTH, GLA_RANK, GLA_KEY), GLA_RANK),
        "b_alpha": 0.1 * jax.random.normal(ks[5], (DEPTH, GLA_KEY), f32),
        "gla_head_norm": gain(ks[6], (DEPTH, GLA_HEADS, GLA_DV)),
        "mix_conv_w": w(ks[7], (DEPTH, CONV_K, CONV_DIM), CONV_K),
        "w_out": w(ks[8], (DEPTH, D_MIX, D_MODEL), D_MIX),
        "ffn_norm": gain(ks[9], (DEPTH, D_MODEL)),
        "w_up": w(ks[10], (DEPTH, D_MODEL, 2 * D_FF), D_MODEL),
        "ffn_conv_w": w(ks[11], (DEPTH, FFN_K, D_FF), FFN_K),
        "w_down": w(ks[12], (DEPTH, D_FF, D_MODEL), D_FF),
        "ple_norm": gain(ks[13], (DEPTH, D_MODEL)),
        "w_ple_gate": w(ks[14], (DEPTH, D_MODEL, D_MODEL), D_MODEL),
        "w_ple_proj": w(ks[15], (DEPTH, PLE_DIM, D_MODEL), PLE_DIM),
        "final_norm": gain(ks[16], (D_MODEL,)),
    }


def reference(x, p, attn_norm, w_in, w_alpha_up, b_alpha, gla_head_norm, mix_conv_w, w_out,
              ffn_norm, w_up, ffn_conv_w, w_down, ple_norm, w_ple_gate, w_ple_proj, final_norm):
    Bn, S, _ = x.shape
    split_idx = [int(i) for i in np.cumsum(IN_SPLIT_SIZES)[:-1]]
    h = x
    for i in range(DEPTH):
        y = rmsnorm(h, attn_norm[i])
        z = y @ w_in[i]
        q, k, v, g, a_low, cb, cc, cx = jnp.split(z, split_idx, axis=-1)

        a_logit = (a_low @ w_alpha_up[i] + b_alpha[i]).astype(jnp.float32)
        log_a = jax.nn.log_sigmoid(a_logit) / GLA_TAU
        o = gla_chunked(q.reshape(Bn, S, GLA_HEADS, GLA_DK),
                        k.reshape(Bn, S, GLA_HEADS, GLA_DK),
                        v.reshape(Bn, S, GLA_HEADS, GLA_DV),
                        log_a.reshape(Bn, S, GLA_HEADS, GLA_DK))
        o = rmsnorm(o, gla_head_norm[i]).astype(h.dtype)
        o = o * jax.nn.silu(g.reshape(Bn, S, GLA_HEADS, GLA_DV))
        gla_out = o.reshape(Bn, S, GLA_VAL)

        conv_out = cb * causal_dwconv(cc * cx, mix_conv_w[i])

        mix = jnp.concatenate([gla_out, conv_out], axis=-1)
        h = h + mix @ w_out[i]

        y = rmsnorm(h, ffn_norm[i])
        gate, up = jnp.split(y @ w_up[i], 2, axis=-1)
        gate = causal_dwconv(gate, ffn_conv_w[i])
        h = h + (jax.nn.silu(gate) * up) @ w_down[i]

        y = rmsnorm(h, ple_norm[i])
        ple_gate = jax.nn.sigmoid(y @ w_ple_gate[i])
        h = h + ple_gate * (p[i].astype(h.dtype) @ w_ple_proj[i])
    return rmsnorm(h, final_norm)
```

```python
import functools

import jax
import jax.numpy as jnp
from jax import lax
from jax.experimental import pallas as pl
from jax.experimental.pallas import tpu as pltpu

D_MODEL = 1024
PLE_DIM = 256
GLA_HEADS = 4
GLA_DK = 64
GLA_DV = 128
GLA_KEY = GLA_HEADS * GLA_DK
GLA_VAL = GLA_HEADS * GLA_DV
GLA_RANK = 16
GLA_TAU = 16.0
GLA_CHUNK = 64
CONV_DIM = 512
CONV_K = 3
D_MIX = GLA_VAL + CONV_DIM
D_FF = 2816
EPS = 1e-6

LANES = 128
SUBLANES = 8
RANK_PAD = LANES

OFF_Q = 0
OFF_K = OFF_Q + GLA_KEY
OFF_V = OFF_K + GLA_KEY
OFF_G = OFF_V + GLA_VAL
OFF_A = OFF_G + GLA_VAL
OFF_CB = OFF_A + RANK_PAD
OFF_CC = OFF_CB + CONV_DIM
OFF_CX = OFF_CC + CONV_DIM
D_IN_PAD = OFF_CX + CONV_DIM

CUM_ROWS = 256
HALO = SUBLANES

MIX_TILE = 512
FFN_TILE = 512
VMEM_LIMIT = 56 * 1024 * 1024


def _rms_scale(x):
    return lax.rsqrt(jnp.mean(x * x, axis=-1, keepdims=True) + EPS)


def _sigmoid(x):
    return 1.0 / (1.0 + jnp.exp(-x))


def _log_sigmoid(x):
    return -(jnp.maximum(-x, 0.0) + jnp.log1p(jnp.exp(-jnp.abs(x))))


def _dot(a, b):
    return jnp.dot(a, b, preferred_element_type=jnp.float32)


def _dot_nt(a, b):
    return lax.dot_general(a, b, (((1,), (1,)), ((), ())), preferred_element_type=jnp.float32)


def _dot_tn(a, b):
    return lax.dot_general(a, b, (((0,), (0,)), ((), ())), preferred_element_type=jnp.float32)


def _split3(x):
    hi = x.astype(jnp.bfloat16)
    r = x - hi.astype(jnp.float32)
    mid = r.astype(jnp.bfloat16)
    lo = (r - mid.astype(jnp.float32)).astype(jnp.bfloat16)
    return hi, mid, lo


def _mixer_kernel(x_ref, an_ref, win_ref, wau_ref, ba_ref, hn_ref, cw_ref, wout_ref, o_ref,
                  z_ref, b_ref, mix_ref, st_ref, u_ref):
    T = x_ref.shape[0]
    bf16 = jnp.bfloat16

    @pl.when(pl.program_id(1) == 0)
    def _():
        st_ref[...] = jnp.zeros_like(st_ref)
        u_ref[0:HALO, :] = jnp.zeros((HALO, CONV_DIM), jnp.float32)

    x = x_ref[...]
    y = (x * _rms_scale(x)) * an_ref[...]
    z_ref[...] = _dot(y.astype(bf16), win_ref[...])

    a_low = z_ref[:, OFF_A:OFF_A + RANK_PAD]
    a_logit = _dot(a_low.astype(bf16), wau_ref[...]) + ba_ref[...]
    log_a = _log_sigmoid(a_logit) * (1.0 / GLA_TAU)
    ri = lax.broadcasted_iota(jnp.int32, (CUM_ROWS, CUM_ROWS), 0)
    ci = lax.broadcasted_iota(jnp.int32, (CUM_ROWS, CUM_ROWS), 1)
    tri = jnp.where((ri // GLA_CHUNK == ci // GLA_CHUNK) & (ci <= ri), 1.0, 0.0).astype(bf16)
    for g in range(T // CUM_ROWS):
        rows = slice(g * CUM_ROWS, (g + 1) * CUM_ROWS)
        hi, mid, lo = _split3(log_a[rows])
        b_ref[rows, :] = (_dot(tri, hi) + _dot(tri, mid)) + _dot(tri, lo)

    ri = lax.broadcasted_iota(jnp.int32, (GLA_CHUNK, GLA_CHUNK), 0)
    ci = lax.broadcasted_iota(jnp.int32, (GLA_CHUNK, GLA_CHUNK), 1)
    causal = ci <= ri

    def chunk_body(c, carry):
        r0 = pl.multiple_of(c * GLA_CHUNK, GLA_CHUNK)
        rows = pl.ds(r0, GLA_CHUNK)
        b = b_ref[rows, :]
        b_last = b[GLA_CHUNK - 1:GLA_CHUNK, :]
        q_d = (z_ref[rows, OFF_Q:OFF_Q + GLA_KEY] * (GLA_DK ** -0.5)) * jnp.exp(b)
        k = z_ref[rows, OFF_K:OFF_K + GLA_KEY]
        k_d = k * jnp.exp(-b)
        k_end = k * jnp.exp(b_last - b)
        decay = jnp.exp(b_last)
        for h in range(GLA_HEADS):
            ks = slice(h * GLA_DK, (h + 1) * GLA_DK)
            vs = slice(h * GLA_DV, (h + 1) * GLA_DV)
            qh = q_d[:, ks].astype(bf16)
            vh = z_ref[rows, OFF_V + h * GLA_DV:OFF_V + (h + 1) * GLA_DV].astype(bf16)
            scores = jnp.where(causal, _dot_nt(qh, k_d[:, ks].astype(bf16)), 0.0)
            st = st_ref[h]
            o = _dot(scores.astype(bf16), vh) + _dot_nt(qh, st.astype(bf16))
            st_ref[h] = st * decay[:, ks] + _dot_tn(vh, k_end[:, ks].astype(bf16))
            o = (o * _rms_scale(o)) * hn_ref[:, vs]
            gate = z_ref[rows, OFF_G + h * GLA_DV:OFF_G + (h + 1) * GLA_DV]
            mix_ref[rows, vs] = (o * (gate * _sigmoid(gate))).astype(bf16)
        return carry

    lax.fori_loop(0, T // GLA_CHUNK, chunk_body, 0)

    u_ref[HALO:HALO + T, :] = z_ref[:, OFF_CC:OFF_CC + CONV_DIM] * z_ref[:, OFF_CX:OFF_CX + CONV_DIM]
    conv = cw_ref[2:3, :] * u_ref[HALO:HALO + T, :]
    conv = conv + cw_ref[0:1, :] * u_ref[HALO - 2:HALO - 2 + T, :]
    conv = conv + cw_ref[1:2, :] * u_ref[HALO - 1:HALO - 1 + T, :]
    mix_ref[:, GLA_VAL:D_MIX] = (z_ref[:, OFF_CB:OFF_CB + CONV_DIM] * conv).astype(bf16)
    u_ref[0:HALO, :] = u_ref[T:T + HALO, :]

    o_ref[...] = x_ref[...] + _dot(mix_ref[...], wout_ref[...])


def _ffn_kernel(h_ref, p_ref, fn_ref, wup_ref, fcw_ref, wdn_ref, pn_ref, wg_ref, wp_ref, on_ref, o_ref,
                gu_ref, act_ref):
    T = h_ref.shape[0]
    bf16 = jnp.bfloat16

    @pl.when(pl.program_id(1) == 0)
    def _():
        gu_ref[0:HALO, :] = jnp.zeros((HALO, 2 * D_FF), jnp.float32)

    h = h_ref[...]
    y = (h * _rms_scale(h)) * fn_ref[...]
    gu_ref[HALO:HALO + T, :] = _dot(y.astype(bf16), wup_ref[...])

    cb = 2 * LANES
    for j in range(D_FF // cb):
        cols = slice(j * cb, (j + 1) * cb)
        ucols = slice(D_FF + j * cb, D_FF + (j + 1) * cb)
        conv = fcw_ref[2:3, cols] * gu_ref[HALO:HALO + T, cols]
        conv = conv + fcw_ref[0:1, cols] * gu_ref[HALO - 2:HALO - 2 + T, cols]
        conv = conv + fcw_ref[1:2, cols] * gu_ref[HALO - 1:HALO - 1 + T, cols]
        act_ref[:, cols] = ((conv * _sigmoid(conv)) * gu_ref[HALO:HALO + T, ucols]).astype(bf16)
    gu_ref[0:HALO, 0:D_FF] = gu_ref[T:T + HALO, 0:D_FF]

    h2 = h + _dot(act_ref[...], wdn_ref[...])

    y = (h2 * _rms_scale(h2)) * pn_ref[...]
    ple_gate = _sigmoid(_dot(y.astype(bf16), wg_ref[...]))
    h3 = h2 + ple_gate * _dot(p_ref[...].astype(bf16), wp_ref[...])
    o_ref[...] = (h3 * _rms_scale(h3)) * on_ref[...]


def _const_spec(shape):
    return pl.BlockSpec(shape, lambda b, s: (0,) * len(shape), pipeline_mode=pl.Buffered(1))


def _tile_spec(tile, width):
    return pl.BlockSpec((None, tile, width), lambda b, s: (b, s, 0))


def _mixer(x, attn_norm, w_in_p, w_au_p, b_alpha, head_norm, conv_w, w_out):
    B, S, _ = x.shape
    T = MIX_TILE
    return pl.pallas_call(
        _mixer_kernel,
        grid=(B, S // T),
        in_specs=[
            _tile_spec(T, D_MODEL),
            _const_spec((1, D_MODEL)),
            _const_spec((D_MODEL, D_IN_PAD)),
            _const_spec((RANK_PAD, GLA_KEY)),
            _const_spec((1, GLA_KEY)),
            _const_spec((1, GLA_VAL)),
            _const_spec((CONV_K, CONV_DIM)),
            _const_spec((D_MIX, D_MODEL)),
        ],
        out_specs=_tile_spec(T, D_MODEL),
        out_shape=jax.ShapeDtypeStruct(x.shape, jnp.float32),
        scratch_shapes=[
            pltpu.VMEM((T, D_IN_PAD), jnp.float32),
            pltpu.VMEM((T, GLA_KEY), jnp.float32),
            pltpu.VMEM((T, D_MIX), jnp.bfloat16),
            pltpu.VMEM((GLA_HEADS, GLA_DV, GLA_DK), jnp.float32),
            pltpu.VMEM((T + HALO, CONV_DIM), jnp.float32),
        ],
        compiler_params=pltpu.CompilerParams(
            dimension_semantics=("arbitrary", "arbitrary"), vmem_limit_bytes=VMEM_LIMIT),
        name="mixer",
    )(x, attn_norm, w_in_p, w_au_p, b_alpha, head_norm, conv_w, w_out)


def _ffn(h, p, ffn_norm, w_up, ffn_conv_w, w_down, ple_norm, w_g, w_p, final_norm):
    B, S, _ = h.shape
    T = FFN_TILE
    return pl.pallas_call(
        _ffn_kernel,
        grid=(B, S // T),
        in_specs=[
            _tile_spec(T, D_MODEL),
            _tile_spec(T, PLE_DIM),
            _const_spec((1, D_MODEL)),
            _const_spec((D_MODEL, 2 * D_FF)),
            _const_spec((CONV_K, D_FF)),
            _const_spec((D_FF, D_MODEL)),
            _const_spec((1, D_MODEL)),
            _const_spec((D_MODEL, D_MODEL)),
            _const_spec((PLE_DIM, D_MODEL)),
            _const_spec((1, D_MODEL)),
        ],
        out_specs=_tile_spec(T, D_MODEL),
        out_shape=jax.ShapeDtypeStruct(h.shape, jnp.float32),
        scratch_shapes=[
            pltpu.VMEM((T + HALO, 2 * D_FF), jnp.float32),
            pltpu.VMEM((T, D_FF), jnp.bfloat16),
        ],
        compiler_params=pltpu.CompilerParams(
            dimension_semantics=("arbitrary", "arbitrary"), vmem_limit_bytes=VMEM_LIMIT),
        name="ffn",
    )(h, p, ffn_norm, w_up, ffn_conv_w, w_down, ple_norm, w_g, w_p, final_norm)


def _repack_w_in(w):
    a0 = OFF_A
    a1 = a0 + GLA_RANK
    pad = jnp.zeros((w.shape[0], RANK_PAD - GLA_RANK), w.dtype)
    return jnp.concatenate([w[:, :a0], w[:, a0:a1], pad, w[:, a1:]], axis=1)


def kernel(x, p, attn_norm, w_in, w_alpha_up, b_alpha, gla_head_norm, mix_conv_w, w_out, ffn_norm, w_up,
           ffn_conv_w, w_down, ple_norm, w_ple_gate, w_ple_proj, final_norm):
    bf16 = jnp.bfloat16
    assert w_in.shape[0] == 1, "kernel is written for DEPTH == 1"
    w_in_p = _repack_w_in(w_in[0]).astype(bf16)
    w_au_p = jnp.pad(w_alpha_up[0], ((0, RANK_PAD - GLA_RANK), (0, 0))).astype(bf16)
    h = _mixer(x, attn_norm, w_in_p, w_au_p, b_alpha, gla_head_norm.reshape(1, GLA_VAL), mix_conv_w[0],
               w_out[0].astype(bf16))
    return _ffn(h, p[0], ffn_norm, w_up[0].astype(bf16), ffn_conv_w[0], w_down[0].astype(bf16), ple_norm,
                w_ple_gate[0].astype(bf16), w_ple_proj[0].astype(bf16), final_norm[None])
```

```python
import functools

import jax
import jax.numpy as jnp
from jax import lax
from jax.experimental import pallas as pl
from jax.experimental.pallas import tpu as pltpu

D_MODEL = 1024
PLE_DIM = 256
GLA_HEADS = 4
GLA_DK = 64
GLA_DV = 128
GLA_KEY = GLA_HEADS * GLA_DK
GLA_VAL = GLA_HEADS * GLA_DV
GLA_RANK = 16
GLA_TAU = 16.0
GLA_CHUNK = 64
CONV_DIM = 512
CONV_K = 3
D_MIX = GLA_VAL + CONV_DIM
D_FF = 2816
EPS = 1e-6

LANES = 128
SUBLANES = 8
MXU_COLS = 256
RANK_PAD = LANES

OFF_A = 0
OFF_Q = OFF_A + RANK_PAD
OFF_K = OFF_Q + GLA_KEY
OFF_V = OFF_K + GLA_KEY
OFF_G = OFF_V + GLA_VAL
OFF_C = OFF_G + GLA_VAL
D_IN_PAD = OFF_C + 3 * CONV_DIM

CUM_ROWS = 256
HALO = SUBLANES

MIX_TILE = 512
FFN_TILE = 512
VMEM_LIMIT = 56 * 1024 * 1024


def _rms_scale(x):
    return lax.rsqrt(jnp.mean(x * x, axis=-1, keepdims=True) + EPS)


def _sigmoid(x):
    return 1.0 / (1.0 + jnp.exp(-x))


def _log_sigmoid(x):
    return -(jnp.maximum(-x, 0.0) + jnp.log1p(jnp.exp(-jnp.abs(x))))


def _dot(a, b):
    return jnp.dot(a, b, preferred_element_type=jnp.float32)


def _dot_nt(a, b):
    return lax.dot_general(a, b, (((1,), (1,)), ((), ())), preferred_element_type=jnp.float32)


def _dot_tn(a, b):
    return lax.dot_general(a, b, (((0,), (0,)), ((), ())), preferred_element_type=jnp.float32)


def _split3(x):
    hi = x.astype(jnp.bfloat16)
    r = x - hi.astype(jnp.float32)
    mid = r.astype(jnp.bfloat16)
    lo = (r - mid.astype(jnp.float32)).astype(jnp.bfloat16)
    return hi, mid, lo


def _interleave(major, minor):
    out, taken = [], 0
    for i, item in enumerate(major):
        out.append(item)
        want = (i + 1) * len(minor) // len(major)
        out.extend(minor[taken:want])
        taken = want
    return out


_HANDOFF = ("x", "g", "qm", "kd", "ke", "vb", "dec", "mix")


def _mixer_kernel(x_ref, an_ref, win_ref, wau_ref, ba_ref, hn_ref, cw_ref, wout_ref, o_ref, *scratch,
                  tiles_per_seq):
    n = len(_HANDOFF)
    buf_a = dict(zip(_HANDOFF, scratch[:n]))
    buf_b = dict(zip(_HANDOFF, scratch[n:2 * n]))
    y_ref, zt_ref, zc_ref, b_ref, u_ref, st_ref, ut_ref, o2_ref = scratch[2 * n:]
    t = pl.program_id(0)
    prev = jnp.maximum(t - 1, 0)

    @pl.when(t == 0)
    def _():
        for r in buf_b.values():
            r[...] = jnp.zeros_like(r)

    @pl.when(t % tiles_per_seq == 0)
    def _():
        u_ref[0:HALO, :] = jnp.zeros((HALO, CONV_DIM), jnp.float32)

    @pl.when(prev % tiles_per_seq == 0)
    def _():
        st_ref[...] = jnp.zeros_like(st_ref)

    shared = (an_ref, win_ref, wau_ref, ba_ref, hn_ref, cw_ref, wout_ref,
              y_ref, zt_ref, zc_ref, b_ref, u_ref, st_ref, ut_ref, o2_ref)

    @pl.when(t % 2 == 0)
    def _():
        _mixer_step(x_ref, o_ref, buf_a, buf_b, *shared)

    @pl.when(t % 2 == 1)
    def _():
        _mixer_step(x_ref, o_ref, buf_b, buf_a, *shared)


def _mixer_step(x_ref, o_ref, new, old, an_ref, win_ref, wau_ref, ba_ref, hn_ref, cw_ref, wout_ref,
                y_ref, zt_ref, zc_ref, b_ref, u_ref, st_ref, ut_ref, o2_ref):
    T = x_ref.shape[0]
    n_chunks = T // GLA_CHUNK
    bf16 = jnp.bfloat16

    x = x_ref[...]
    new["x"][...] = x
    y_ref[...] = ((x * _rms_scale(x)) * an_ref[...]).astype(bf16)

    def project(dst_ref, dst_col, w_col, width):
        def emit():
            dst_ref[:, dst_col:dst_col + width] = _dot(y_ref[...], win_ref[:, w_col:w_col + width])
        return emit

    def gate_logits():
        a_low = zt_ref[:, OFF_A:OFF_A + RANK_PAD]
        a_logit = _dot(a_low.astype(bf16), wau_ref[...]) + ba_ref[...]
        b_ref[...] = _log_sigmoid(a_logit) * (1.0 / GLA_TAU)

    def gate_cumsum():
        ri = lax.broadcasted_iota(jnp.int32, (CUM_ROWS, CUM_ROWS), 0)
        ci = lax.broadcasted_iota(jnp.int32, (CUM_ROWS, CUM_ROWS), 1)
        tri = jnp.where((ri // GLA_CHUNK == ci // GLA_CHUNK) & (ci <= ri), 1.0, 0.0).astype(bf16)
        for g in range(T // CUM_ROWS):
            rows = slice(g * CUM_ROWS, (g + 1) * CUM_ROWS)
            hi, mid, lo = _split3(b_ref[rows, :])
            b_ref[rows, :] = (_dot(tri, hi) + _dot(tri, mid)) + _dot(tri, lo)

    def decayed_qk():
        head_of_lane = lax.broadcasted_iota(jnp.int32, (GLA_CHUNK, GLA_KEY), 1) // GLA_DK
        for c in range(n_chunks):
            rows = slice(c * GLA_CHUNK, (c + 1) * GLA_CHUNK)
            b = b_ref[rows, :]
            b_last = b[GLA_CHUNK - 1:GLA_CHUNK, :]
            q_d = (zt_ref[rows, OFF_Q:OFF_Q + GLA_KEY] * (GLA_DK ** -0.5)) * jnp.exp(b)
            for h in range(GLA_HEADS):
                new["qm"][h, rows, :] = jnp.where(head_of_lane == h, q_d, 0.0).astype(bf16)
            k = zt_ref[rows, OFF_K:OFF_K + GLA_KEY]
            new["kd"][rows, :] = (k * jnp.exp(-b)).astype(bf16)
            new["ke"][rows, :] = (k * jnp.exp(b_last - b)).astype(bf16)
            new["dec"][c:c + 1, :] = jnp.exp(b_last)

    def values_bf16():
        new["vb"][...] = zt_ref[:, OFF_V:OFF_V + GLA_VAL].astype(bf16)

    def short_conv():
        u_ref[HALO:HALO + T, :] = zc_ref[:, CONV_DIM:2 * CONV_DIM] * zc_ref[:, 2 * CONV_DIM:3 * CONV_DIM]
        conv = cw_ref[2:3, :] * u_ref[HALO:HALO + T, :]
        conv = conv + cw_ref[0:1, :] * u_ref[HALO - 2:HALO - 2 + T, :]
        conv = conv + cw_ref[1:2, :] * u_ref[HALO - 1:HALO - 1 + T, :]
        new["mix"][:, GLA_VAL:D_MIX] = (zc_ref[:, 0:CONV_DIM] * conv).astype(bf16)
        u_ref[0:HALO, :] = u_ref[T:T + HALO, :]

    proj = [project(zt_ref, OFF_A, OFF_A, RANK_PAD),
            project(zt_ref, OFF_Q, OFF_Q, GLA_KEY),
            gate_logits,
            project(zt_ref, OFF_K, OFF_K, GLA_KEY),
            gate_cumsum]
    proj += [project(zt_ref, OFF_V + j, OFF_V + j, MXU_COLS) for j in range(0, GLA_VAL, MXU_COLS)]
    proj += [decayed_qk, values_bf16]
    proj += [project(new["g"], j, OFF_G + j, MXU_COLS) for j in range(0, GLA_VAL, MXU_COLS)]
    proj += [project(zc_ref, j, OFF_C + j, MXU_COLS) for j in range(0, 3 * CONV_DIM, MXU_COLS)]
    proj += [short_conv]

    ri = lax.broadcasted_iota(jnp.int32, (GLA_CHUNK, GLA_CHUNK), 0)
    ci = lax.broadcasted_iota(jnp.int32, (GLA_CHUNK, GLA_CHUNK), 1)
    causal = ci <= ri
    masked_scores = {}

    def chunk_scores(c):
        def emit():
            rows = slice(c * GLA_CHUNK, (c + 1) * GLA_CHUNK)
            for h in range(GLA_HEADS):
                s = _dot_nt(old["qm"][h, rows, :], old["kd"][rows, :])
                masked_scores[c, h] = jnp.where(causal, s, 0.0).astype(bf16)
                vh = old["vb"][rows, h * GLA_DV:(h + 1) * GLA_DV]
                ut_ref[c % 2, h] = _dot_tn(vh, old["ke"][rows, :])
        return emit

    def chunk_output(c):
        def emit():
            rows = slice(c * GLA_CHUNK, (c + 1) * GLA_CHUNK)
            decay = old["dec"][c:c + 1, :]
            for h in range(GLA_HEADS):
                vs = slice(h * GLA_DV, (h + 1) * GLA_DV)
                st = st_ref[h]
                o = _dot(masked_scores[c, h], old["vb"][rows, vs]) + _dot_nt(old["qm"][h, rows, :], st.astype(bf16))
                o2_ref[rows, vs] = o
                st_ref[h] = st * decay + ut_ref[c % 2, h]
        return emit

    def head_norm_gate():
        for h in range(GLA_HEADS):
            vs = slice(h * GLA_DV, (h + 1) * GLA_DV)
            o = o2_ref[:, vs]
            o = (o * _rms_scale(o)) * hn_ref[:, vs]
            gate = old["g"][:, vs]
            old["mix"][:, vs] = (o * (gate * _sigmoid(gate))).astype(bf16)

    def out_proj():
        o_ref[...] = old["x"][...] + _dot(old["mix"][...], wout_ref[...])

    fin = [chunk_scores(0)]
    for c in range(n_chunks):
        if c + 1 < n_chunks:
            fin.append(chunk_scores(c + 1))
        fin.append(chunk_output(c))
    fin.append(head_norm_gate)

    for emit in _interleave(fin, proj):
        emit()
    out_proj()


def _ffn_kernel(h_ref, p_ref, fn_ref, wup_ref, fcw_ref, wdn_ref, pn_ref, wg_ref, wp_ref, on_ref, o_ref,
                gu_ref, act_ref):
    T = h_ref.shape[0]
    bf16 = jnp.bfloat16

    @pl.when(pl.program_id(1) == 0)
    def _():
        gu_ref[0:HALO, :] = jnp.zeros((HALO, 2 * D_FF), jnp.float32)

    h = h_ref[...]
    y = (h * _rms_scale(h)) * fn_ref[...]
    gu_ref[HALO:HALO + T, :] = _dot(y.astype(bf16), wup_ref[...])

    cb = 2 * LANES
    for j in range(D_FF // cb):
        cols = slice(j * cb, (j + 1) * cb)
        ucols = slice(D_FF + j * cb, D_FF + (j + 1) * cb)
        conv = fcw_ref[2:3, cols] * gu_ref[HALO:HALO + T, cols]
        conv = conv + fcw_ref[0:1, cols] * gu_ref[HALO - 2:HALO - 2 + T, cols]
        conv = conv + fcw_ref[1:2, cols] * gu_ref[HALO - 1:HALO - 1 + T, cols]
        act_ref[:, cols] = ((conv * _sigmoid(conv)) * gu_ref[HALO:HALO + T, ucols]).astype(bf16)
    gu_ref[0:HALO, 0:D_FF] = gu_ref[T:T + HALO, 0:D_FF]

    h2 = h + _dot(act_ref[...], wdn_ref[...])

    y = (h2 * _rms_scale(h2)) * pn_ref[...]
    ple_gate = _sigmoid(_dot(y.astype(bf16), wg_ref[...]))
    h3 = h2 + ple_gate * _dot(p_ref[...].astype(bf16), wp_ref[...])
    o_ref[...] = (h3 * _rms_scale(h3)) * on_ref[...]


def _const_spec(shape):
    zeros = (0,) * len(shape)
    return pl.BlockSpec(shape, lambda *g: zeros, pipeline_mode=pl.Buffered(1))


def _tile_spec(tile, width):
    return pl.BlockSpec((None, tile, width), lambda b, s: (b, s, 0))


def _mixer(x, attn_norm, w_in_p, w_au_p, b_alpha, head_norm, conv_w, w_out):
    B, S, _ = x.shape
    T = MIX_TILE
    nseq = S // T
    ntiles = B * nseq
    f32, bf16 = jnp.float32, jnp.bfloat16

    def x_map(t):
        i = jnp.minimum(t, ntiles - 1)
        return (i // nseq, i % nseq, 0)

    def o_map(t):
        i = jnp.maximum(t - 1, 0)
        return (i // nseq, i % nseq, 0)

    handoff = dict(
        x=pltpu.VMEM((T, D_MODEL), f32),
        g=pltpu.VMEM((T, GLA_VAL), f32),
        qm=pltpu.VMEM((GLA_HEADS, T, GLA_KEY), bf16),
        kd=pltpu.VMEM((T, GLA_KEY), bf16),
        ke=pltpu.VMEM((T, GLA_KEY), bf16),
        vb=pltpu.VMEM((T, GLA_VAL), bf16),
        dec=pltpu.VMEM((T // GLA_CHUNK, GLA_KEY), f32),
        mix=pltpu.VMEM((T, D_MIX), bf16),
    )
    handoff = [handoff[k] for k in _HANDOFF]
    return pl.pallas_call(
        functools.partial(_mixer_kernel, tiles_per_seq=nseq),
        grid=(ntiles + 1,),
        in_specs=[
            pl.BlockSpec((None, T, D_MODEL), x_map),
            _const_spec((1, D_MODEL)),
            _const_spec((D_MODEL, D_IN_PAD)),
            _const_spec((RANK_PAD, GLA_KEY)),
            _const_spec((1, GLA_KEY)),
            _const_spec((1, GLA_VAL)),
            _const_spec((CONV_K, CONV_DIM)),
            _const_spec((D_MIX, D_MODEL)),
        ],
        out_specs=pl.BlockSpec((None, T, D_MODEL), o_map),
        out_shape=jax.ShapeDtypeStruct(x.shape, f32),
        scratch_shapes=handoff + handoff + [
            pltpu.VMEM((T, D_MODEL), bf16),
            pltpu.VMEM((T, OFF_G), f32),
            pltpu.VMEM((T, 3 * CONV_DIM), f32),
            pltpu.VMEM((T, GLA_KEY), f32),
            pltpu.VMEM((T + HALO, CONV_DIM), f32),
            pltpu.VMEM((GLA_HEADS, GLA_DV, GLA_KEY), f32),
            pltpu.VMEM((2, GLA_HEADS, GLA_DV, GLA_KEY), f32),
            pltpu.VMEM((T, GLA_VAL), f32),
        ],
        compiler_params=pltpu.CompilerParams(
            dimension_semantics=("arbitrary",), vmem_limit_bytes=VMEM_LIMIT),
        name="mixer",
    )(x, attn_norm, w_in_p, w_au_p, b_alpha, head_norm, conv_w, w_out)


def _ffn(h, p, ffn_norm, w_up, ffn_conv_w, w_down, ple_norm, w_g, w_p, final_norm):
    B, S, _ = h.shape
    T = FFN_TILE
    return pl.pallas_call(
        _ffn_kernel,
        grid=(B, S // T),
        in_specs=[
            _tile_spec(T, D_MODEL),
            _tile_spec(T, PLE_DIM),
            _const_spec((1, D_MODEL)),
            _const_spec((D_MODEL, 2 * D_FF)),
            _const_spec((CONV_K, D_FF)),
            _const_spec((D_FF, D_MODEL)),
            _const_spec((1, D_MODEL)),
            _const_spec((D_MODEL, D_MODEL)),
            _const_spec((PLE_DIM, D_MODEL)),
            _const_spec((1, D_MODEL)),
        ],
        out_specs=_tile_spec(T, D_MODEL),
        out_shape=jax.ShapeDtypeStruct(h.shape, jnp.float32),
        scratch_shapes=[
            pltpu.VMEM((T + HALO, 2 * D_FF), jnp.float32),
            pltpu.VMEM((T, D_FF), jnp.bfloat16),
        ],
        compiler_params=pltpu.CompilerParams(
            dimension_semantics=("arbitrary", "arbitrary"), vmem_limit_bytes=VMEM_LIMIT),
        name="ffn",
    )(h, p, ffn_norm, w_up, ffn_conv_w, w_down, ple_norm, w_g, w_p, final_norm)


def _repack_w_in(w):
    qkvg = GLA_KEY + GLA_KEY + GLA_VAL + GLA_VAL
    pad = jnp.zeros((w.shape[0], RANK_PAD - GLA_RANK), w.dtype)
    return jnp.concatenate([w[:, qkvg:qkvg + GLA_RANK], pad, w[:, :qkvg], w[:, qkvg + GLA_RANK:]], axis=1)


def kernel(x, p, attn_norm, w_in, w_alpha_up, b_alpha, gla_head_norm, mix_conv_w, w_out, ffn_norm, w_up,
           ffn_conv_w, w_down, ple_norm, w_ple_gate, w_ple_proj, final_norm):
    bf16 = jnp.bfloat16
    assert w_in.shape[0] == 1, "kernel is written for DEPTH == 1"
    w_in_p = _repack_w_in(w_in[0]).astype(bf16)
    w_au_p = jnp.pad(w_alpha_up[0], ((0, RANK_PAD - GLA_RANK), (0, 0))).astype(bf16)
    h = _mixer(x, attn_norm, w_in_p, w_au_p, b_alpha, gla_head_norm.reshape(1, GLA_VAL), mix_conv_w[0],
               w_out[0].astype(bf16))
    return _ffn(h, p[0], ffn_norm, w_up[0].astype(bf16), ffn_conv_w[0], w_down[0].astype(bf16), ple_norm,
                w_ple_gate[0].astype(bf16), w_ple_proj[0].astype(bf16), final_norm[None])
```

```python
import functools

import jax
import jax.numpy as jnp
from jax import lax
from jax.experimental import pallas as pl
from jax.experimental.pallas import tpu as pltpu

D_MODEL = 1024
PLE_DIM = 256
GLA_HEADS = 4
GLA_DK = 64
GLA_DV = 128
GLA_KEY = GLA_HEADS * GLA_DK
GLA_VAL = GLA_HEADS * GLA_DV
GLA_RANK = 16
GLA_TAU = 16.0
GLA_CHUNK = 64
CONV_DIM = 512
CONV_K = 3
D_MIX = GLA_VAL + CONV_DIM
D_FF = 2816
EPS = 1e-6

LANES = 128
SUBLANES = 8
MXU_COLS = 256
RANK_PAD = LANES

OFF_A = 0
OFF_Q = OFF_A + RANK_PAD
OFF_K = OFF_Q + GLA_KEY
OFF_V = OFF_K + GLA_KEY
OFF_G = OFF_V + GLA_VAL
OFF_C = OFF_G + GLA_VAL
D_IN_PAD = OFF_C + 3 * CONV_DIM

CUM_ROWS = 256
HALO = SUBLANES

MIX_TILE = 512
FFN_TILE = 512
VMEM_LIMIT = 56 * 1024 * 1024


def _rms_scale(x):
    return lax.rsqrt(jnp.mean(x * x, axis=-1, keepdims=True) + EPS)


def _sigmoid(x):
    return 1.0 / (1.0 + jnp.exp(-x))


def _log_sigmoid(x):
    return -(jnp.maximum(-x, 0.0) + jnp.log1p(jnp.exp(-jnp.abs(x))))


def _dot(a, b):
    return jnp.dot(a, b, preferred_element_type=jnp.float32)


def _dot_nt(a, b):
    return lax.dot_general(a, b, (((1,), (1,)), ((), ())), preferred_element_type=jnp.float32)


def _dot_tn(a, b):
    return lax.dot_general(a, b, (((0,), (0,)), ((), ())), preferred_element_type=jnp.float32)


def _split3(x):
    hi = x.astype(jnp.bfloat16)
    r = x - hi.astype(jnp.float32)
    mid = r.astype(jnp.bfloat16)
    lo = (r - mid.astype(jnp.float32)).astype(jnp.bfloat16)
    return hi, mid, lo


def _interleave(major, minor):
    out, taken = [], 0
    for i, item in enumerate(major):
        out.append(item)
        want = (i + 1) * len(minor) // len(major)
        out.extend(minor[taken:want])
        taken = want
    return out


_HANDOFF = ("x", "g", "qm", "kd", "ke", "vb", "dec", "mix")


def _mixer_kernel(x_ref, an_ref, win_ref, wau_ref, ba_ref, hn_ref, cw_ref, wout_ref, o_ref, *scratch,
                  tiles_per_seq):
    n = len(_HANDOFF)
    buf_a = dict(zip(_HANDOFF, scratch[:n]))
    buf_b = dict(zip(_HANDOFF, scratch[n:2 * n]))
    y_ref, zt_ref, zc_ref, b_ref, u_ref, st_ref, ut_ref, o2_ref = scratch[2 * n:]
    t = pl.program_id(0)
    prev = jnp.maximum(t - 1, 0)

    @pl.when(t == 0)
    def _():
        for r in buf_b.values():
            r[...] = jnp.zeros_like(r)

    @pl.when(t % tiles_per_seq == 0)
    def _():
        u_ref[0:HALO, :] = jnp.zeros((HALO, CONV_DIM), jnp.float32)

    @pl.when(prev % tiles_per_seq == 0)
    def _():
        st_ref[...] = jnp.zeros_like(st_ref)

    shared = (an_ref, win_ref, wau_ref, ba_ref, hn_ref, cw_ref, wout_ref,
              y_ref, zt_ref, zc_ref, b_ref, u_ref, st_ref, ut_ref, o2_ref)

    @pl.when(t % 2 == 0)
    def _():
        _mixer_step(x_ref, o_ref, buf_a, buf_b, *shared)

    @pl.when(t % 2 == 1)
    def _():
        _mixer_step(x_ref, o_ref, buf_b, buf_a, *shared)


def _mixer_step(x_ref, o_ref, new, old, an_ref, win_ref, wau_ref, ba_ref, hn_ref, cw_ref, wout_ref,
                y_ref, zt_ref, zc_ref, b_ref, u_ref, st_ref, ut_ref, o2_ref):
    T = x_ref.shape[0]
    n_chunks = T // GLA_CHUNK
    bf16 = jnp.bfloat16

    x = x_ref[...]
    new["x"][...] = x
    y_ref[...] = ((x * _rms_scale(x)) * an_ref[...]).astype(bf16)

    def project(dst_ref, dst_col, w_col, width):
        def emit():
            dst_ref[:, dst_col:dst_col + width] = _dot(y_ref[...], win_ref[:, w_col:w_col + width])
        return emit

    def gate_logits():
        a_low = zt_ref[:, OFF_A:OFF_A + RANK_PAD]
        a_logit = _dot(a_low.astype(bf16), wau_ref[...]) + ba_ref[...]
        b_ref[...] = _log_sigmoid(a_logit) * (1.0 / GLA_TAU)

    def gate_cumsum():
        ri = lax.broadcasted_iota(jnp.int32, (CUM_ROWS, CUM_ROWS), 0)
        ci = lax.broadcasted_iota(jnp.int32, (CUM_ROWS, CUM_ROWS), 1)
        tri = jnp.where((ri // GLA_CHUNK == ci // GLA_CHUNK) & (ci <= ri), 1.0, 0.0).astype(bf16)
        for g in range(T // CUM_ROWS):
            rows = slice(g * CUM_ROWS, (g + 1) * CUM_ROWS)
            hi, mid, lo = _split3(b_ref[rows, :])
            b_ref[rows, :] = (_dot(tri, hi) + _dot(tri, mid)) + _dot(tri, lo)

    def decayed_qk():
        head_of_lane = lax.broadcasted_iota(jnp.int32, (GLA_CHUNK, GLA_KEY), 1) // GLA_DK
        for c in range(n_chunks):
            rows = slice(c * GLA_CHUNK, (c + 1) * GLA_CHUNK)
            b = b_ref[rows, :]
            b_last = b[GLA_CHUNK - 1:GLA_CHUNK, :]
            q_d = (zt_ref[rows, OFF_Q:OFF_Q + GLA_KEY] * (GLA_DK ** -0.5)) * jnp.exp(b)
            for h in range(GLA_HEADS):
                new["qm"][h, rows, :] = jnp.where(head_of_lane == h, q_d, 0.0).astype(bf16)
            k = zt_ref[rows, OFF_K:OFF_K + GLA_KEY]
            new["kd"][rows, :] = (k * jnp.exp(-b)).astype(bf16)
            new["ke"][rows, :] = (k * jnp.exp(b_last - b)).astype(bf16)
            new["dec"][c:c + 1, :] = jnp.exp(b_last)

    def values_bf16():
        new["vb"][...] = zt_ref[:, OFF_V:OFF_V + GLA_VAL].astype(bf16)

    def short_conv():
        u_ref[HALO:HALO + T, :] = zc_ref[:, CONV_DIM:2 * CONV_DIM] * zc_ref[:, 2 * CONV_DIM:3 * CONV_DIM]
        conv = cw_ref[2:3, :] * u_ref[HALO:HALO + T, :]
        conv = conv + cw_ref[0:1, :] * u_ref[HALO - 2:HALO - 2 + T, :]
        conv = conv + cw_ref[1:2, :] * u_ref[HALO - 1:HALO - 1 + T, :]
        new["mix"][:, GLA_VAL:D_MIX] = (zc_ref[:, 0:CONV_DIM] * conv).astype(bf16)
        u_ref[0:HALO, :] = u_ref[T:T + HALO, :]

    proj = [project(zt_ref, OFF_A, OFF_A, RANK_PAD),
            project(zt_ref, OFF_Q, OFF_Q, GLA_KEY),
            gate_logits,
            project(zt_ref, OFF_K, OFF_K, GLA_KEY),
            gate_cumsum]
    proj += [project(zt_ref, OFF_V + j, OFF_V + j, MXU_COLS) for j in range(0, GLA_VAL, MXU_COLS)]
    proj += [decayed_qk, values_bf16]
    proj += [project(new["g"], j, OFF_G + j, MXU_COLS) for j in range(0, GLA_VAL, MXU_COLS)]
    proj += [project(zc_ref, j, OFF_C + j, MXU_COLS) for j in range(0, 3 * CONV_DIM, MXU_COLS)]
    proj += [short_conv]

    ri = lax.broadcasted_iota(jnp.int32, (GLA_CHUNK, GLA_CHUNK), 0)
    ci = lax.broadcasted_iota(jnp.int32, (GLA_CHUNK, GLA_CHUNK), 1)
    causal = ci <= ri
    masked_scores = {}

    def chunk_scores(c):
        def emit():
            rows = slice(c * GLA_CHUNK, (c + 1) * GLA_CHUNK)
            for h in range(GLA_HEADS):
                s = _dot_nt(old["qm"][h, rows, :], old["kd"][rows, :])
                masked_scores[c, h] = jnp.where(causal, s, 0.0).astype(bf16)
                vh = old["vb"][rows, h * GLA_DV:(h + 1) * GLA_DV]
                ut_ref[c % 2, h] = _dot_tn(vh, old["ke"][rows, :])
        return emit

    def chunk_output(c):
        def emit():
            rows = slice(c * GLA_CHUNK, (c + 1) * GLA_CHUNK)
            decay = old["dec"][c:c + 1, :]
            for h in range(GLA_HEADS):
                vs = slice(h * GLA_DV, (h + 1) * GLA_DV)
                st = st_ref[h]
                o = _dot(masked_scores[c, h], old["vb"][rows, vs]) + _dot_nt(old["qm"][h, rows, :], st.astype(bf16))
                o2_ref[rows, vs] = o
                st_ref[h] = st * decay + ut_ref[c % 2, h]
        return emit

    def head_norm_gate():
        for h in range(GLA_HEADS):
            vs = slice(h * GLA_DV, (h + 1) * GLA_DV)
            o = o2_ref[:, vs]
            o = (o * _rms_scale(o)) * hn_ref[:, vs]
            gate = old["g"][:, vs]
            old["mix"][:, vs] = (o * (gate * _sigmoid(gate))).astype(bf16)

    def out_proj():
        o_ref[...] = old["x"][...] + _dot(old["mix"][...], wout_ref[...])

    fin = [chunk_scores(0)]
    for c in range(n_chunks):
        if c + 1 < n_chunks:
            fin.append(chunk_scores(c + 1))
        fin.append(chunk_output(c))
    fin.append(head_norm_gate)

    for emit in _interleave(fin, proj):
        emit()
    out_proj()


def _ffn_kernel(h_ref, p_ref, fn_ref, wup_ref, fcw_ref, wdn_ref, pn_ref, wg_ref, wp_ref, on_ref, o_ref,
                h2a_ref, h2b_ref, gu_ref, act_ref, *, tiles_per_seq):
    t = pl.program_id(0)

    @pl.when(t == 0)
    def _():
        h2b_ref[...] = jnp.zeros_like(h2b_ref)

    @pl.when(t % tiles_per_seq == 0)
    def _():
        gu_ref[0:HALO, :] = jnp.zeros((HALO, 2 * D_FF), jnp.float32)

    shared = (fn_ref, wup_ref, fcw_ref, wdn_ref, pn_ref, wg_ref, wp_ref, on_ref, gu_ref, act_ref)

    @pl.when(t % 2 == 0)
    def _():
        _ffn_step(h_ref, p_ref, o_ref, h2a_ref, h2b_ref, *shared)

    @pl.when(t % 2 == 1)
    def _():
        _ffn_step(h_ref, p_ref, o_ref, h2b_ref, h2a_ref, *shared)


def _ffn_step(h_ref, p_ref, o_ref, h2_new_ref, h2_old_ref,
              fn_ref, wup_ref, fcw_ref, wdn_ref, pn_ref, wg_ref, wp_ref, on_ref, gu_ref, act_ref):
    T = h_ref.shape[0]
    bf16 = jnp.bfloat16

    ple = _dot(p_ref[...].astype(bf16), wp_ref[...])
    h2 = h2_old_ref[...]
    y = (h2 * _rms_scale(h2)) * pn_ref[...]
    ple_gate = _sigmoid(_dot(y.astype(bf16), wg_ref[...]))
    h3 = h2 + ple_gate * ple
    o_ref[...] = (h3 * _rms_scale(h3)) * on_ref[...]

    h = h_ref[...]
    y = (h * _rms_scale(h)) * fn_ref[...]
    gu_ref[HALO:HALO + T, :] = _dot(y.astype(bf16), wup_ref[...])

    cb = 2 * LANES
    for j in range(D_FF // cb):
        cols = slice(j * cb, (j + 1) * cb)
        ucols = slice(D_FF + j * cb, D_FF + (j + 1) * cb)
        conv = fcw_ref[2:3, cols] * gu_ref[HALO:HALO + T, cols]
        conv = conv + fcw_ref[0:1, cols] * gu_ref[HALO - 2:HALO - 2 + T, cols]
        conv = conv + fcw_ref[1:2, cols] * gu_ref[HALO - 1:HALO - 1 + T, cols]
        act_ref[:, cols] = ((conv * _sigmoid(conv)) * gu_ref[HALO:HALO + T, ucols]).astype(bf16)
    gu_ref[0:HALO, 0:D_FF] = gu_ref[T:T + HALO, 0:D_FF]

    h2_new_ref[...] = h + _dot(act_ref[...], wdn_ref[...])


def _const_spec(shape):
    zeros = (0,) * len(shape)
    return pl.BlockSpec(shape, lambda *g: zeros, pipeline_mode=pl.Buffered(1))


def _skewed_tile_maps(ntiles, nseq):
    def cur(t):
        i = jnp.minimum(t, ntiles - 1)
        return (i // nseq, i % nseq, 0)

    def prev(t):
        i = jnp.maximum(t - 1, 0)
        return (i // nseq, i % nseq, 0)

    return cur, prev


def _mixer(x, attn_norm, w_in_p, w_au_p, b_alpha, head_norm, conv_w, w_out):
    B, S, _ = x.shape
    T = MIX_TILE
    nseq = S // T
    ntiles = B * nseq
    f32, bf16 = jnp.float32, jnp.bfloat16

    x_map, o_map = _skewed_tile_maps(ntiles, nseq)

    handoff = dict(
        x=pltpu.VMEM((T, D_MODEL), f32),
        g=pltpu.VMEM((T, GLA_VAL), f32),
        qm=pltpu.VMEM((GLA_HEADS, T, GLA_KEY), bf16),
        kd=pltpu.VMEM((T, GLA_KEY), bf16),
        ke=pltpu.VMEM((T, GLA_KEY), bf16),
        vb=pltpu.VMEM((T, GLA_VAL), bf16),
        dec=pltpu.VMEM((T // GLA_CHUNK, GLA_KEY), f32),
        mix=pltpu.VMEM((T, D_MIX), bf16),
    )
    handoff = [handoff[k] for k in _HANDOFF]
    return pl.pallas_call(
        functools.partial(_mixer_kernel, tiles_per_seq=nseq),
        grid=(ntiles + 1,),
        in_specs=[
            pl.BlockSpec((None, T, D_MODEL), x_map),
            _const_spec((1, D_MODEL)),
            _const_spec((D_MODEL, D_IN_PAD)),
            _const_spec((RANK_PAD, GLA_KEY)),
            _const_spec((1, GLA_KEY)),
            _const_spec((1, GLA_VAL)),
            _const_spec((CONV_K, CONV_DIM)),
            _const_spec((D_MIX, D_MODEL)),
        ],
        out_specs=pl.BlockSpec((None, T, D_MODEL), o_map),
        out_shape=jax.ShapeDtypeStruct(x.shape, f32),
        scratch_shapes=handoff + handoff + [
            pltpu.VMEM((T, D_MODEL), bf16),
            pltpu.VMEM((T, OFF_G), f32),
            pltpu.VMEM((T, 3 * CONV_DIM), f32),
            pltpu.VMEM((T, GLA_KEY), f32),
            pltpu.VMEM((T + HALO, CONV_DIM), f32),
            pltpu.VMEM((GLA_HEADS, GLA_DV, GLA_KEY), f32),
            pltpu.VMEM((2, GLA_HEADS, GLA_DV, GLA_KEY), f32),
            pltpu.VMEM((T, GLA_VAL), f32),
        ],
        compiler_params=pltpu.CompilerParams(
            dimension_semantics=("arbitrary",), vmem_limit_bytes=VMEM_LIMIT),
        name="mixer",
    )(x, attn_norm, w_in_p, w_au_p, b_alpha, head_norm, conv_w, w_out)


def _ffn(h, p, ffn_norm, w_up, ffn_conv_w, w_down, ple_norm, w_g, w_p, final_norm):
    B, S, _ = h.shape
    T = FFN_TILE
    nseq = S // T
    ntiles = B * nseq
    cur, prev = _skewed_tile_maps(ntiles, nseq)
    return pl.pallas_call(
        functools.partial(_ffn_kernel, tiles_per_seq=nseq),
        grid=(ntiles + 1,),
        in_specs=[
            pl.BlockSpec((None, T, D_MODEL), cur),
            pl.BlockSpec((None, T, PLE_DIM), prev),
            _const_spec((1, D_MODEL)),
            _const_spec((D_MODEL, 2 * D_FF)),
            _const_spec((CONV_K, D_FF)),
            _const_spec((D_FF, D_MODEL)),
            _const_spec((1, D_MODEL)),
            _const_spec((D_MODEL, D_MODEL)),
            _const_spec((PLE_DIM, D_MODEL)),
            _const_spec((1, D_MODEL)),
        ],
        out_specs=pl.BlockSpec((None, T, D_MODEL), prev),
        out_shape=jax.ShapeDtypeStruct(h.shape, jnp.float32),
        scratch_shapes=[
            pltpu.VMEM((T, D_MODEL), jnp.float32),
            pltpu.VMEM((T, D_MODEL), jnp.float32),
            pltpu.VMEM((T + HALO, 2 * D_FF), jnp.float32),
            pltpu.VMEM((T, D_FF), jnp.bfloat16),
        ],
        compiler_params=pltpu.CompilerParams(
            dimension_semantics=("arbitrary",), vmem_limit_bytes=VMEM_LIMIT),
        name="ffn",
    )(h, p, ffn_norm, w_up, ffn_conv_w, w_down, ple_norm, w_g, w_p, final_norm)


def _repack_w_in(w):
    qkvg = GLA_KEY + GLA_KEY + GLA_VAL + GLA_VAL
    pad = jnp.zeros((w.shape[0], RANK_PAD - GLA_RANK), w.dtype)
    return jnp.concatenate([w[:, qkvg:qkvg + GLA_RANK], pad, w[:, :qkvg], w[:, qkvg + GLA_RANK:]], axis=1)


def kernel(x, p, attn_norm, w_in, w_alpha_up, b_alpha, gla_head_norm, mix_conv_w, w_out, ffn_norm, w_up,
           ffn_conv_w, w_down, ple_norm, w_ple_gate, w_ple_proj, final_norm):
    bf16 = jnp.bfloat16
    assert w_in.shape[0] == 1, "kernel is written for DEPTH == 1"
    w_in_p = _repack_w_in(w_in[0]).astype(bf16)
    w_au_p = jnp.pad(w_alpha_up[0], ((0, RANK_PAD - GLA_RANK), (0, 0))).astype(bf16)
    h = _mixer(x, attn_norm, w_in_p, w_au_p, b_alpha, gla_head_norm.reshape(1, GLA_VAL), mix_conv_w[0],
               w_out[0].astype(bf16))
    return _ffn(h, p[0], ffn_norm, w_up[0].astype(bf16), ffn_conv_w[0], w_down[0].astype(bf16), ple_norm,
                w_ple_gate[0].astype(bf16), w_ple_proj[0].astype(bf16), final_norm[None])
```
